```python
import math
import jax
import jax.numpy as jnp
from jax import lax
import numpy as np

D_MODEL = 4096
BATCH = 4
SEQ = 4096
DEPTH = 4

CTX_LEN = 256
GRID_W = 64
BLOCK = 128
WINDOW = 128
ROPE_BASE = 10000.0
NORM_EPS = 1e-6
N_MOD = 6

HEAD_DIM = 128
SWA_HEADS = 12
SWA_KV_HEADS = 4
MLA_HEADS = 10
MLA_Q_RANK = 768
MLA_KV_RANK = 512
MLA_NOPE = 128
MLA_ROPE = 64
MLA_V = 128
DIF_HEADS = 5
DIF_QK = 128
DIF_V = 256

MIX_WIDTH = SWA_HEADS * HEAD_DIM + MLA_HEADS * MLA_V + DIF_HEADS * DIF_V
IN_SIZES = (SWA_HEADS * HEAD_DIM, SWA_KV_HEADS * HEAD_DIM, SWA_KV_HEADS * HEAD_DIM,
            MLA_Q_RANK, MLA_KV_RANK, MLA_ROPE,
            DIF_HEADS * 2 * DIF_QK, DIF_HEADS * 2 * DIF_QK, DIF_HEADS * DIF_V)
IN_WIDTH = sum(IN_SIZES)

D_FF = 4096
N_EXPERTS = 8
TOP_K = 2
D_FF_EXPERT = 1024
N_DENSE = (DEPTH + 1) // 2
N_MOE = DEPTH // 2

kernel_name = 'hybrid_group_dit_trunk'


def _rms_norm(x, g):
    xf = x.astype(jnp.float32)
    y = xf * lax.rsqrt(jnp.mean(xf * xf, axis=-1, keepdims=True) + NORM_EPS)
    return (y * g.astype(jnp.float32)).astype(x.dtype)


def _modulate(x, shift, scale):
    return x * (1 + scale) + shift


def _axial_rope(t_row, t_col, dim):
    quarter = dim // 4
    inv = ROPE_BASE ** (-jnp.arange(quarter, dtype=jnp.float32) / quarter)
    ang = jnp.concatenate([t_row[:, None].astype(jnp.float32) * inv,
                           t_col[:, None].astype(jnp.float32) * inv], axis=-1)
    return jnp.cos(ang), jnp.sin(ang)


def _rope(x, cos, sin):
    half = x.shape[-1] // 2
    idx = (None, slice(None)) + (None,) * (x.ndim - 3) + (slice(None),)
    cs, sn = cos[idx], sin[idx]
    x1 = x[..., :half].astype(jnp.float32)
    x2 = x[..., half:].astype(jnp.float32)
    return jnp.concatenate([x1 * cs - x2 * sn, x1 * sn + x2 * cs], axis=-1).astype(x.dtype)


def _split_cols(z):
    out, start = [], 0
    for w in IN_SIZES:
        out.append(z[..., start:start + w])
        start += w
    return out


def _swa_project(q_cols, k_cols, v_cols, q_g, k_g, rope):
    b, n = q_cols.shape[0], q_cols.shape[1]
    q = _rms_norm(q_cols.reshape(b, n, SWA_HEADS, HEAD_DIM), q_g)
    k = _rms_norm(k_cols.reshape(b, n, SWA_KV_HEADS, HEAD_DIM), k_g)
    v = v_cols.reshape(b, n, SWA_KV_HEADS, HEAD_DIM)
    if rope is not None:
        q = _rope(q, *rope)
        k = _rope(k, *rope)
    return q, k, v


def _mla_project(cq, ckv, kr, cq_g, ckv_g, w_uq, w_ukv, q_g, k_g, rope):
    b, n = cq.shape[0], cq.shape[1]
    q = (_rms_norm(cq, cq_g) @ w_uq).reshape(b, n, MLA_HEADS, MLA_NOPE + MLA_ROPE)
    kv = (_rms_norm(ckv, ckv_g) @ w_ukv).reshape(b, n, MLA_HEADS, MLA_NOPE + MLA_V)
    k_nope, v = kv[..., :MLA_NOPE], kv[..., MLA_NOPE:]
    k = jnp.concatenate([k_nope, jnp.broadcast_to(kr[:, :, None, :], (b, n, MLA_HEADS, MLA_ROPE))], axis=-1)
    q = _rms_norm(q, q_g)
    k = _rms_norm(k, k_g)
    if rope is not None:
        q = jnp.concatenate([q[..., :MLA_NOPE], _rope(q[..., MLA_NOPE:], *rope)], axis=-1)
        k = jnp.concatenate([k[..., :MLA_NOPE], _rope(k[..., MLA_NOPE:], *rope)], axis=-1)
    return q, k, v


def _dif_project(q_cols, k_cols, v_cols, q_g, k_g, rope):
    b, n = q_cols.shape[0], q_cols.shape[1]
    q = _rms_norm(q_cols.reshape(b, n, DIF_HEADS, 2, DIF_QK), q_g)
    k = _rms_norm(k_cols.reshape(b, n, DIF_HEADS, 2, DIF_QK), k_g)
    v = v_cols.reshape(b, n, DIF_HEADS, DIF_V)
    if rope is not None:
        q = _rope(q, *rope)
        k = _rope(k, *rope)
    return q, k, v


def _window_gqa(q, k, v, kc, vc, sink):
    b, s = q.shape[0], q.shape[1]
    nb = s // BLOCK
    rep = SWA_HEADS // SWA_KV_HEADS
    scale = HEAD_DIM ** -0.5
    qb = q.reshape(b, nb, BLOCK, SWA_KV_HEADS, rep, HEAD_DIM)

    def bands(t):
        tp = jnp.pad(t, ((0, 0), (BLOCK, BLOCK), (0, 0), (0, 0)))
        tp = tp.reshape(b, nb + 2, BLOCK, SWA_KV_HEADS, HEAD_DIM)
        return jnp.concatenate([tp[:, :-2], tp[:, 1:-1], tp[:, 2:]], axis=2)

    kw, vw = bands(k), bands(v)
    qpos = jnp.arange(s).reshape(nb, BLOCK)
    kpos = jnp.arange(nb)[:, None] * BLOCK + jnp.arange(3 * BLOCK)[None, :] - BLOCK
    valid = ((jnp.abs(qpos[:, :, None] - kpos[:, None, :]) <= WINDOW)
             & (kpos >= 0)[:, None, :] & (kpos < s)[:, None, :])
    s_win = jnp.einsum('bnqgrd,bnkgd->bngrqk', qb, kw).astype(jnp.float32) * scale
    s_win = jnp.where(valid[None, :, None, None], s_win, -jnp.inf)
    s_ctx = jnp.einsum('bnqgrd,bcgd->bngrqc', qb, kc).astype(jnp.float32) * scale
    s_sink = jnp.broadcast_to(sink.astype(jnp.float32).reshape(1, 1, SWA_KV_HEADS, rep, 1, 1),
                              s_win.shape[:-1] + (1,))
    p = jax.nn.softmax(jnp.concatenate([s_win, s_ctx, s_sink], axis=-1), axis=-1).astype(v.dtype)
    nw, nc = 3 * BLOCK, kc.shape[1]
    out = (jnp.einsum('bngrqk,bnkgd->bnqgrd', p[..., :nw], vw)
           + jnp.einsum('bngrqc,bcgd->bnqgrd', p[..., nw:nw + nc], vc))
    return out.reshape(b, s, SWA_HEADS * HEAD_DIM)


def _ctx_gqa(qc, kc, vc, sink):
    b, n = qc.shape[0], qc.shape[1]
    rep = SWA_HEADS // SWA_KV_HEADS
    qg = qc.reshape(b, n, SWA_KV_HEADS, rep, HEAD_DIM)
    sc = jnp.einsum('bqgrd,bkgd->bgrqk', qg, kc).astype(jnp.float32) * (HEAD_DIM ** -0.5)
    s_sink = jnp.broadcast_to(sink.astype(jnp.float32).reshape(1, SWA_KV_HEADS, rep, 1, 1),
                              sc.shape[:-1] + (1,))
    p = jax.nn.softmax(jnp.concatenate([sc, s_sink], axis=-1), axis=-1)[..., :n].astype(vc.dtype)
    return jnp.einsum('bgrqk,bkgd->bqgrd', p, vc).reshape(b, n, SWA_HEADS * HEAD_DIM)


def _attend(q, k, v):
    s = jnp.einsum('bqhd,bkhd->bhqk', q, k).astype(jnp.float32) * (q.shape[-1] ** -0.5)
    p = jax.nn.softmax(s, axis=-1).astype(v.dtype)
    return jnp.einsum('bhqk,bkhe->bqhe', p, v)


def _diff_attend(q, k, v, lam):
    s = jnp.einsum('bqhmd,bkhmd->bhmqk', q, k).astype(jnp.float32) * (DIF_QK ** -0.5)
    p = jax.nn.softmax(s, axis=-1)
    a = (p[:, :, 0] - lam * p[:, :, 1]).astype(v.dtype)
    return jnp.einsum('bhqk,bkhe->bqhe', a, v)


def _sweep_query_blocks(fn, q):
    b, s = q.shape[0], q.shape[1]
    nb = s // BLOCK
    qb = jnp.moveaxis(q.reshape((b, nb, BLOCK) + q.shape[2:]), 1, 0)
    out = lax.map(fn, qb)
    return jnp.moveaxis(out, 0, 1).reshape((b, s) + out.shape[3:])


def _merge(y_swa, y_mla, y_dif, subln_g, lam_init):
    b, n = y_swa.shape[0], y_swa.shape[1]
    y_dif = _rms_norm(y_dif, subln_g) * (1 - lam_init)
    return jnp.concatenate([y_swa.reshape(b, n, -1), y_mla.reshape(b, n, -1),
                            y_dif.reshape(b, n, -1)], axis=-1)


def _swiglu(x, w1, w3, w2):
    return (jax.nn.silu(x @ w1) * (x @ w3)) @ w2


def _moe(x, router, w1, w3, w2):
    logits = (x @ router).astype(jnp.float32)
    top_v, top_i = lax.top_k(logits, TOP_K)
    gates = jax.nn.softmax(top_v, axis=-1)
    combine = jnp.sum(jax.nn.one_hot(top_i, N_EXPERTS, dtype=jnp.float32) * gates[..., None],
                      axis=-2).astype(x.dtype)
    y = jnp.zeros_like(x)
    for e in range(N_EXPERTS):
        y = y + combine[..., e:e + 1] * _swiglu(x, w1[e], w3[e], w2[e])
    return y


def _channel_mixer(l, x, ffn_w1, ffn_w3, ffn_w2, moe_router, moe_w1, moe_w3, moe_w2):
    i = l // 2
    if l % 2 == 0:
        return _swiglu(x, ffn_w1[i], ffn_w3[i], ffn_w2[i])
    return _moe(x, moe_router[i], moe_w1[i], moe_w3[i], moe_w2[i])


def setup_inputs(seed: int = 0) -> dict:
    key = jax.random.key(seed)
    k = jax.random.split(key, 30)
    f32 = jnp.float32

    def nrm(kk, shape, scale):
        return jax.random.normal(kk, shape, f32) * scale

    def gain(kk, shape):
        return 1.0 + 0.02 * jax.random.normal(kk, shape, f32)

    L, D = DEPTH, D_MODEL
    return {
        'x': nrm(k[0], (BATCH, SEQ, D), 1.0),
        'c': nrm(k[1], (BATCH, D), 1.0),
        'ctx': nrm(k[2], (BATCH, CTX_LEN, D), 1.0),
        'c_ctx': nrm(k[3], (D,), 1.0),
        'ada_w': nrm(k[4], (L, D, N_MOD * D), 0.5 * D ** -0.5),
        'ada_b': nrm(k[5], (L, N_MOD * D), 0.02),
        'norm1_g': gain(k[6], (L, D)),
        'norm2_g': gain(k[7], (L, D)),
        'w_in': nrm(k[8], (L, D, IN_WIDTH), D ** -0.5),
        'w_out': nrm(k[9], (L, MIX_WIDTH, D), MIX_WIDTH ** -0.5),
        'swa_sink': nrm(k[10], (L, SWA_HEADS), 0.5),
        'swa_q_norm': gain(k[11], (L, HEAD_DIM)),
        'swa_k_norm': gain(k[12], (L, HEAD_DIM)),
        'mla_cq_norm': gain(k[13], (L, MLA_Q_RANK)),
        'mla_ckv_norm': gain(k[14], (L, MLA_KV_RANK)),
        'mla_w_uq': nrm(k[15], (L, MLA_Q_RANK, MLA_HEADS * (MLA_NOPE + MLA_ROPE)), MLA_Q_RANK ** -0.5),
        'mla_w_ukv': nrm(k[16], (L, MLA_KV_RANK, MLA_HEADS * (MLA_NOPE + MLA_V)), MLA_KV_RANK ** -0.5),
        'mla_q_norm': gain(k[17], (L, MLA_NOPE + MLA_ROPE)),
        'mla_k_norm': gain(k[18], (L, MLA_NOPE + MLA_ROPE)),
        'dif_lambda': nrm(k[19], (L, 4, DIF_QK), 0.1),
        'dif_q_norm': gain(k[20], (L, 2, DIF_QK)),
        'dif_k_norm': gain(k[21], (L, 2, DIF_QK)),
        'dif_subln': gain(k[22], (L, DIF_V)),
        'ffn_w1': nrm(k[23], (N_DENSE, D, D_FF), D ** -0.5),
        'ffn_w3': nrm(k[24], (N_DENSE, D, D_FF), D ** -0.5),
        'ffn_w2': nrm(k[25], (N_DENSE, D_FF, D), D_FF ** -0.5),
        'moe_router': nrm(k[26], (N_MOE, D, N_EXPERTS), D ** -0.5),
        'moe_w1': nrm(k[27], (N_MOE, N_EXPERTS, D, D_FF_EXPERT), D ** -0.5),
        'moe_w3': nrm(k[28], (N_MOE, N_EXPERTS, D, D_FF_EXPERT), D ** -0.5),
        'moe_w2': nrm(k[29], (N_MOE, N_EXPERTS, D_FF_EXPERT, D), D_FF_EXPERT ** -0.5),
    }


def reference(x, c, ctx, c_ctx, ada_w, ada_b, norm1_g, norm2_g, w_in, w_out,
              swa_sink, swa_q_norm, swa_k_norm,
              mla_cq_norm, mla_ckv_norm, mla_w_uq, mla_w_ukv, mla_q_norm, mla_k_norm,
              dif_lambda, dif_q_norm, dif_k_norm, dif_subln,
              ffn_w1, ffn_w3, ffn_w2, moe_router, moe_w1, moe_w3, moe_w2):
    b, s, d = x.shape
    rows = s // GRID_W
    t_row = jnp.broadcast_to(jnp.arange(rows)[:, None], (rows, GRID_W)).reshape(-1)
    t_col = jnp.broadcast_to(jnp.arange(GRID_W)[None, :], (rows, GRID_W)).reshape(-1)
    rope_head = _axial_rope(t_row, t_col, HEAD_DIM)
    rope_mla = _axial_rope(t_row, t_col, MLA_ROPE)
    silu_c = jax.nn.silu(c)
    silu_cc = jax.nn.silu(c_ctx)
    h, hc = x, ctx
    for l in range(DEPTH):
        last = l == DEPTH - 1
        mod = (silu_c @ ada_w[l] + ada_b[l]).reshape(b, 1, N_MOD, d)
        modc = (silu_cc @ ada_w[l] + ada_b[l]).reshape(N_MOD, d)
        sh1, sc1, g1, sh2, sc2, g2 = [mod[:, :, i] for i in range(N_MOD)]
        csh1, csc1, cg1, csh2, csc2, cg2 = [modc[i] for i in range(N_MOD)]

        lat = _split_cols(_modulate(_rms_norm(h, norm1_g[l]), sh1, sc1) @ w_in[l])
        cx = _split_cols(_modulate(_rms_norm(hc, norm1_g[l]), csh1, csc1) @ w_in[l])

        sq, sk, sv = _swa_project(lat[0], lat[1], lat[2], swa_q_norm[l], swa_k_norm[l], rope_head)
        sqc, skc, svc = _swa_project(cx[0], cx[1], cx[2], swa_q_norm[l], swa_k_norm[l], None)
        mq, mk, mv = _mla_project(lat[3], lat[4], lat[5], mla_cq_norm[l], mla_ckv_norm[l],
                                  mla_w_uq[l], mla_w_ukv[l], mla_q_norm[l], mla_k_norm[l], rope_mla)
        mqc, mkc, mvc = _mla_project(cx[3], cx[4], cx[5], mla_cq_norm[l], mla_ckv_norm[l],
                                     mla_w_uq[l], mla_w_ukv[l], mla_q_norm[l], mla_k_norm[l], None)
        dq, dk, dv = _dif_project(lat[6], lat[7], lat[8], dif_q_norm[l], dif_k_norm[l], rope_head)
        dqc, dkc, dvc = _dif_project(cx[6], cx[7], cx[8], dif_q_norm[l], dif_k_norm[l], None)

        lam_init = 0.8 - 0.6 * math.exp(-0.3 * l)
        lq1, lk1, lq2, lk2 = [dif_lambda[l, i].astype(jnp.float32) for i in range(4)]
        lam = jnp.exp(jnp.sum(lq1 * lk1)) - jnp.exp(jnp.sum(lq2 * lk2)) + lam_init

        mk_all = jnp.concatenate([mk, mkc], axis=1)
        mv_all = jnp.concatenate([mv, mvc], axis=1)
        dk_all = jnp.concatenate([dk, dkc], axis=1)
        dv_all = jnp.concatenate([dv, dvc], axis=1)
        y_swa = _window_gqa(sq, sk, sv, skc, svc, swa_sink[l])
        y_mla = _sweep_query_blocks(lambda qb: _attend(qb, mk_all, mv_all), mq)
        y_dif = _sweep_query_blocks(lambda qb: _diff_attend(qb, dk_all, dv_all, lam), dq)
        y = _merge(y_swa, y_mla, y_dif, dif_subln[l], lam_init) @ w_out[l]
        h_mixed = h + g1 * y

        if not last:
            yc = _merge(_ctx_gqa(sqc, skc, svc, swa_sink[l]), _attend(mqc, mkc, mvc),
                        _diff_attend(dqc, dkc, dvc, lam), dif_subln[l], lam_init) @ w_out[l]
            hc = hc + cg1 * yc
        h = h_mixed

        hn = _modulate(_rms_norm(h, norm2_g[l]), sh2, sc2)
        h = h + g2 * _channel_mixer(l, hn, ffn_w1, ffn_w3, ffn_w2, moe_router, moe_w1, moe_w3, moe_w2)
        if not last:
            hcn = _modulate(_rms_norm(hc, norm2_g[l]), csh2, csc2)
            hc = hc + cg2 * _channel_mixer(l, hcn, ffn_w1, ffn_w3, ffn_w2, moe_router, moe_w1, moe_w3, moe_w2)
    return h
```

```python
import functools
import math

import jax
import jax.numpy as jnp
from jax import lax
from jax.experimental import pallas as pl
from jax.experimental.pallas import tpu as pltpu

F32 = jnp.float32
BF16 = jnp.bfloat16

GRID_W = 64
BLOCK = 128
ROPE_BASE = 10000.0
NORM_EPS = 1e-6
N_MOD = 6
HEAD_DIM = 128
SWA_HEADS = 12
SWA_KV_HEADS = 4
SWA_REP = SWA_HEADS // SWA_KV_HEADS
MLA_HEADS = 10
MLA_Q_RANK = 768
MLA_KV_RANK = 512
MLA_NOPE = 128
MLA_ROPE = 64
MLA_V = 128
MLA_QK = MLA_NOPE + MLA_ROPE
DIF_HEADS = 5
DIF_QK = 128
DIF_V = 256
TOP_K = 2
LOG2E = math.log2(math.e)

LANE = 128
MOD_ROWS = 8
VMEM_LIMIT = 56 * 1024 * 1024
NEG = -1e30

U_CQ, U_DQ, U_CKV, U_SQ, U_SK, U_SV, U_DK, U_DV, U_KR, U_END = 0, 6, 16, 20, 32, 36, 40, 50, 60, 64
Z_WIDTH = U_END * LANE


def _cparams(*sem):
    return pltpu.CompilerParams(dimension_semantics=sem, vmem_limit_bytes=VMEM_LIMIT)


def _pick(n, pref, mult=8):
    if n <= pref:
        return n
    t = pref - pref % mult
    while t > mult and n % t:
        t -= mult
    assert n % t == 0, (n, pref)
    return t


def _ada_kernel(c_ref, w_ref, b_ref, o_ref):
    c = c_ref[...]
    a = (c * jax.nn.sigmoid(c)).astype(BF16)
    o_ref[...] = jnp.dot(a, w_ref[...].astype(BF16), preferred_element_type=F32) + b_ref[...]


def _ada_mod(cin, ada_w, ada_b):
    L, D, N = ada_w.shape
    tn = _pick(N, 512, LANE)
    return pl.pallas_call(
        _ada_kernel,
        grid=(L, N // tn),
        in_specs=[pl.BlockSpec((MOD_ROWS, D), lambda l, j: (0, 0)),
                  pl.BlockSpec((None, D, tn), lambda l, j: (l, 0, j)),
                  pl.BlockSpec((None, 1, tn), lambda l, j: (l, 0, j))],
        out_specs=pl.BlockSpec((None, MOD_ROWS, tn), lambda l, j: (l, 0, j)),
        out_shape=jax.ShapeDtypeStruct((L, MOD_ROWS, N), F32),
        compiler_params=_cparams("arbitrary", "arbitrary"),
        name="ada_mod",
    )(cin, ada_w, ada_b.reshape(L, 1, N))


class _Rows:
    def __init__(self, B, S, C, pref):
        self.B, self.S, self.C = B, S, C
        self.T = B * S + B * C
        self.tm = _pick(math.gcd(S, B * C), pref)
        self.n = self.T // self.tm
        self.n_lat = B * S // self.tm

    def mod_index(self, i, k, tm=None):
        return jnp.minimum((i * (tm or self.tm)) // self.S, self.B) * N_MOD + k

    def tile_for(self, k_dim, budget=8 * 1024 * 1024):
        return _pick(self.tm, max(8, budget // (2 * k_dim)))

    def rope_index(self, i):
        per = self.S // self.tm
        return jnp.where(i < self.n_lat, i % per, per)


def _norm_mod_kernel(h_ref, g_ref, sh_ref, sc_ref, *rest, n_experts):
    x = h_ref[...]
    ms = jnp.mean(x * x, axis=-1, keepdims=True)
    y = x * lax.rsqrt(ms + NORM_EPS) * g_ref[...]
    xb = (y * (1.0 + sc_ref[...]) + sh_ref[...]).astype(BF16)
    if not n_experts:
        (o_ref,) = rest
        o_ref[...] = xb
        return
    r_ref, o_ref, comb_ref = rest
    o_ref[...] = xb
    logits = jnp.dot(xb, r_ref[...], preferred_element_type=F32)
    lane = lax.broadcasted_iota(jnp.int32, logits.shape, 1).astype(F32)
    lg = jnp.where(lane < n_experts, logits, -jnp.inf)
    m1 = jnp.max(lg, axis=-1, keepdims=True)
    i1 = jnp.min(jnp.where(lg == m1, lane, float(LANE)), axis=-1, keepdims=True)
    lg2 = jnp.where(lane == i1, -jnp.inf, lg)
    m2 = jnp.max(lg2, axis=-1, keepdims=True)
    i2 = jnp.min(jnp.where(lg2 == m2, lane, float(LANE)), axis=-1, keepdims=True)
    e = jnp.exp(m2 - m1)
    g1 = 1.0 / (1.0 + e)
    comb_ref[...] = jnp.where(lane == i1, g1, 0.0) + jnp.where(lane == i2, e * g1, 0.0)


def _norm_mod(rows, h, g, modr, k_shift, k_scale, router=None, n_experts=0):
    T, D = h.shape
    tm = _pick(rows.tm, 256)
    mod_spec = lambda k: pl.BlockSpec((None, 1, D), lambda i: (rows.mod_index(i, k, tm), 0, 0))
    in_specs = [pl.BlockSpec((tm, D), lambda i: (i, 0)),
                pl.BlockSpec((1, D), lambda i: (0, 0)),
                mod_spec(k_shift), mod_spec(k_scale)]
    out_specs = pl.BlockSpec((tm, D), lambda i: (i, 0))
    out_shape = jax.ShapeDtypeStruct((T, D), BF16)
    args = [h, g.reshape(1, D), modr, modr]
    if n_experts:
        in_specs.append(pl.BlockSpec((D, LANE), lambda i: (0, 0)))
        args.append(router)
        out_specs = (out_specs, pl.BlockSpec((tm, LANE), lambda i: (i, 0)))
        out_shape = (out_shape, jax.ShapeDtypeStruct((T, LANE), F32))
    return pl.pallas_call(
        functools.partial(_norm_mod_kernel, n_experts=n_experts),
        grid=(T // tm,), in_specs=in_specs, out_specs=out_specs, out_shape=out_shape,
        compiler_params=_cparams("arbitrary"), name="norm_mod",
    )(*args)


def _mm_kernel(x_ref, w_ref, o_ref):
    o_ref[...] = jnp.dot(x_ref[...], w_ref[...], preferred_element_type=F32).astype(o_ref.dtype)


def _mm_res_kernel(x_ref, w_ref, h_ref, g_ref, o_ref):
    acc = jnp.dot(x_ref[...], w_ref[...], preferred_element_type=F32)
    o_ref[...] = h_ref[...] + g_ref[...] * acc


def _matmul(rows, x, w, out_dtype, tn_pref=512):
    T, K = x.shape
    N = w.shape[1]
    tm, tn = rows.tm, _pick(N, tn_pref, LANE)
    return pl.pallas_call(
        _mm_kernel, grid=(T // tm, N // tn),
        in_specs=[pl.BlockSpec((tm, K), lambda i, j: (i, 0)),
                  pl.BlockSpec((K, tn), lambda i, j: (0, j))],
        out_specs=pl.BlockSpec((tm, tn), lambda i, j: (i, j)),
        out_shape=jax.ShapeDtypeStruct((T, N), out_dtype),
        compiler_params=_cparams("arbitrary", "arbitrary"), name="matmul",
    )(x, w)


def _matmul_residual(rows, x, w, h, modr, k_gate, tn_pref=512):
    T, K = x.shape
    N = w.shape[1]
    tm, tn = rows.tile_for(K), _pick(N, tn_pref, LANE)
    return pl.pallas_call(
        _mm_res_kernel, grid=(T // tm, N // tn),
        in_specs=[pl.BlockSpec((tm, K), lambda i, j: (i, 0)),
                  pl.BlockSpec((K, tn), lambda i, j: (0, j)),
                  pl.BlockSpec((tm, tn), lambda i, j: (i, j)),
                  pl.BlockSpec((None, 1, tn), lambda i, j: (rows.mod_index(i, k_gate, tm), 0, j))],
        out_specs=pl.BlockSpec((tm, tn), lambda i, j: (i, j)),
        out_shape=jax.ShapeDtypeStruct((T, N), F32),
        compiler_params=_cparams("arbitrary", "arbitrary"), name="matmul_residual",
    )(x, w, h, modr)


def _swiglu_kernel(x_ref, w1_ref, w3_ref, *rest, tiles_per_expert):
    x = x_ref[...]
    a = jnp.dot(x, w1_ref[...], preferred_element_type=F32)
    b = jnp.dot(x, w3_ref[...], preferred_element_type=F32)
    y = a * jax.nn.sigmoid(a) * b
    if tiles_per_expert:
        comb_ref, o_ref = rest
        comb = comb_ref[...]
        e = (pl.program_id(1) // tiles_per_expert).astype(F32)
        lane = lax.broadcasted_iota(jnp.int32, comb.shape, 1).astype(F32)
        y = y * jnp.sum(jnp.where(lane == e, comb, 0.0), axis=-1, keepdims=True)
    else:
        (o_ref,) = rest
    o_ref[...] = y.astype(o_ref.dtype)


def _swiglu(rows, x, w1, w3, comb=None, tn_pref=512):
    T, D = x.shape
    tm = rows.tm
    x_spec = pl.BlockSpec((tm, D), lambda i, j: (i, 0))
    if comb is None:
        F = w1.shape[1]
        tn = _pick(F, tn_pref, LANE)
        w_spec = pl.BlockSpec((D, tn), lambda i, j: (0, j))
        in_specs, args, tpe = [x_spec, w_spec, w_spec], [x, w1, w3], 0
    else:
        E, _, FE = w1.shape
        F = E * FE
        tn = _pick(FE, tn_pref, LANE)
        tpe = FE // tn
        w_spec = pl.BlockSpec((None, D, tn), lambda i, j: (j // tpe, 0, j % tpe))
        in_specs = [x_spec, w_spec, w_spec, pl.BlockSpec((tm, LANE), lambda i, j: (i, 0))]
        args = [x, w1, w3, comb]
    return pl.pallas_call(
        functools.partial(_swiglu_kernel, tiles_per_expert=tpe),
        grid=(T // tm, F // tn), in_specs=in_specs,
        out_specs=pl.BlockSpec((tm, tn), lambda i, j: (i, j)),
        out_shape=jax.ShapeDtypeStruct((T, F), BF16),
        compiler_params=_cparams("arbitrary", "arbitrary"), name="swiglu",
    )(*args)


def _rope_half(y, cos_ref, sin_ref):
    return y * cos_ref[...] + pltpu.roll(y, LANE // 2, 1) * sin_ref[...]


def _headprep_kernel(z_ref, *rest, norm, rope):
    x = z_ref[...]
    rest = list(rest)
    if norm:
        g_ref = rest.pop(0)
        ms = jnp.mean(x * x, axis=-1, keepdims=True)
        x = x * lax.rsqrt(ms + NORM_EPS) * g_ref[...]
    if rope:
        cos_ref, sin_ref = rest.pop(0), rest.pop(0)
        x = _rope_half(x, cos_ref, sin_ref)
    (o_ref,) = rest
    o_ref[...] = x.astype(o_ref.dtype)


def _headprep(rows, z, unit, n_heads, width, gain=None, rope=None):
    T = z.shape[0]
    tm = rows.tm
    off = unit * LANE // width
    assert off * width == unit * LANE
    in_specs = [pl.BlockSpec((tm, width), lambda i, hh: (i, off + hh))]
    args = [z]
    if gain is not None:
        in_specs.append(pl.BlockSpec((1, width), lambda i, hh: (0, hh)))
        args.append(gain)
    if rope is not None:
        t_spec = pl.BlockSpec((tm, width), lambda i, hh: (rows.rope_index(i), 0))
        in_specs += [t_spec, t_spec]
        args += list(rope)
    return pl.pallas_call(
        functools.partial(_headprep_kernel, norm=gain is not None, rope=rope is not None),
        grid=(T // tm, n_heads), in_specs=in_specs,
        out_specs=pl.BlockSpec((None, tm, width), lambda i, hh: (hh, i, 0)),
        out_shape=jax.ShapeDtypeStruct((n_heads, T, width), BF16),
        compiler_params=_cparams("arbitrary", "arbitrary"), name="headprep",
    )(*args)


def _latent_norm(x_ref, g_ref, xn_ref):
    x = x_ref[...]
    ms = jnp.mean(x * x, axis=-1, keepdims=True)
    xn_ref[...] = (x * lax.rsqrt(ms + NORM_EPS) * g_ref[...]).astype(BF16)


def _mla_q_kernel(cq_ref, gcq_ref, w_ref, gq_ref, cos_ref, sin_ref, o_ref, xn_ref):
    @pl.when(pl.program_id(1) == 0)
    def _():
        _latent_norm(cq_ref, gcq_ref, xn_ref)

    u = jnp.dot(xn_ref[...], w_ref[...], preferred_element_type=F32)
    ms = jnp.sum(u * u, axis=-1, keepdims=True) * (1.0 / MLA_QK)
    y = u * lax.rsqrt(ms + NORM_EPS) * gq_ref[...]
    o_ref[:, :LANE] = y[:, :LANE].astype(o_ref.dtype)
    o_ref[:, LANE:] = _rope_half(y[:, LANE:], cos_ref, sin_ref).astype(o_ref.dtype)


def _mla_q_prep(rows, z, gcq, w_uq, gq, rope):
    T = z.shape[0]
    tm = rows.tm
    t_spec = pl.BlockSpec((tm, LANE), lambda i, j: (rows.rope_index(i), 0))
    return pl.pallas_call(
        _mla_q_kernel, grid=(T // tm, MLA_HEADS),
        in_specs=[pl.BlockSpec((tm, MLA_Q_RANK), lambda i, j: (i, U_CQ * LANE // MLA_Q_RANK)),
                  pl.BlockSpec((1, MLA_Q_RANK), lambda i, j: (0, 0)),
                  pl.BlockSpec((MLA_Q_RANK, 2 * LANE), lambda i, j: (0, j)),
                  pl.BlockSpec((1, 2 * LANE), lambda i, j: (0, 0)),
                  t_spec, t_spec],
        out_specs=pl.BlockSpec((None, tm, 2 * LANE), lambda i, j: (j, i, 0)),
        out_shape=jax.ShapeDtypeStruct((MLA_HEADS, T, 2 * LANE), BF16),
        scratch_shapes=[pltpu.VMEM((tm, MLA_Q_RANK), BF16)],
        compiler_params=_cparams("arbitrary", "arbitrary"), name="mla_q_prep",
    )(z, gcq, w_uq, gq, *rope)


def _mla_kv_kernel(ckv_ref, gckv_ref, w_ref, kr_ref, gk_ref, cos_ref, sin_ref, k_ref, v_ref, xn_ref):
    @pl.when(pl.program_id(1) == 0)
    def _():
        _latent_norm(ckv_ref, gckv_ref, xn_ref)

    u = jnp.dot(xn_ref[...], w_ref[...], preferred_element_type=F32)
    kn = u[:, :LANE]
    kr = kr_ref[...]
    ms = (jnp.sum(kn * kn, axis=-1, keepdims=True)
          + jnp.sum(kr * kr, axis=-1, keepdims=True)) * (1.0 / MLA_QK)
    r = lax.rsqrt(ms + NORM_EPS)
    gk = gk_ref[...]
    k_ref[:, :LANE] = (kn * r * gk[:, :LANE]).astype(k_ref.dtype)
    k_ref[:, LANE:] = _rope_half(kr * r * gk[:, LANE:], cos_ref, sin_ref).astype(k_ref.dtype)
    v_ref[...] = u[:, LANE:].astype(v_ref.dtype)


def _mla_kv_prep(rows, z, gckv, w_ukv, gk, rope):
    T = z.shape[0]
    tm = rows.tm
    t_spec = pl.BlockSpec((tm, LANE), lambda i, j: (rows.rope_index(i), 0))
    return pl.pallas_call(
        _mla_kv_kernel, grid=(T // tm, MLA_HEADS),
        in_specs=[pl.BlockSpec((tm, MLA_KV_RANK), lambda i, j: (i, U_CKV * LANE // MLA_KV_RANK)),
                  pl.BlockSpec((1, MLA_KV_RANK), lambda i, j: (0, 0)),
                  pl.BlockSpec((MLA_KV_RANK, 2 * LANE), lambda i, j: (0, j)),
                  pl.BlockSpec((tm, LANE), lambda i, j: (i, U_KR)),
                  pl.BlockSpec((1, 2 * LANE), lambda i, j: (0, 0)),
                  t_spec, t_spec],
        out_specs=(pl.BlockSpec((None, tm, 2 * LANE), lambda i, j: (j, i, 0)),
                   pl.BlockSpec((None, tm, LANE), lambda i, j: (j, i, 0))),
        out_shape=(jax.ShapeDtypeStruct((MLA_HEADS, T, 2 * LANE), BF16),
                   jax.ShapeDtypeStruct((MLA_HEADS, T, LANE), BF16)),
        scratch_shapes=[pltpu.VMEM((tm, MLA_KV_RANK), BF16)],
        compiler_params=_cparams("arbitrary", "arbitrary"), name="mla_kv_prep",
    )(z, gckv, w_ukv, z, gk, *rope)


def _scores(q, k):
    return lax.dot_general(q, k, (((1,), (1,)), ((), ())), preferred_element_type=F32)


def _softmax_start(q, k, v):
    s = _scores(q, k)
    m = jnp.max(s, axis=-1, keepdims=True)
    p = jnp.exp2(s - m)
    return m, jnp.sum(p, axis=-1, keepdims=True), jnp.dot(p.astype(BF16), v, preferred_element_type=F32)


def _softmax_step(q, k, v, m, l, acc):
    s = _scores(q, k)
    m_new = jnp.maximum(m, jnp.max(s, axis=-1, keepdims=True))
    a = jnp.exp2(m - m_new)
    p = jnp.exp2(s - m_new)
    return (m_new, a * l + jnp.sum(p, axis=-1, keepdims=True),
            a * acc + jnp.dot(p.astype(BF16), v, preferred_element_type=F32))


def _stream(q, kl_ref, vl_ref, kc, vc, n_chunks, tk):
    state = _softmax_start(q, kc, vc)
    if n_chunks:
        def body(c, st):
            off = pl.multiple_of(c * tk, tk)
            return _softmax_step(q, kl_ref[pl.ds(off, tk), :], vl_ref[pl.ds(off, tk), :], *st)
        state = lax.fori_loop(0, n_chunks, body, state)
    _, l, acc = state
    return acc / l


def _flash_kernel(*refs, n_chunks, tk):
    if n_chunks:
        q_ref, kc_ref, vc_ref, kl_ref, vl_ref, o_ref = refs
    else:
        q_ref, kc_ref, vc_ref, o_ref = refs
        kl_ref = vl_ref = None
    o_ref[...] = _stream(q_ref[...], kl_ref, vl_ref, kc_ref[...], vc_ref[...], n_chunks, tk).astype(o_ref.dtype)


def _dif_kernel(*refs, n_chunks, tk, lam_init):
    if n_chunks:
        lam_ref, sub_ref, q_ref, kc_ref, vc_ref, kl_ref, vl_ref, o_ref = refs
    else:
        lam_ref, sub_ref, q_ref, kc_ref, vc_ref, o_ref = refs
        kl_ref = None
        vl_ref = None
    lp = lam_ref[...]
    lam = (jnp.exp(jnp.sum(lp[0:1] * lp[1:2], axis=-1, keepdims=True))
           - jnp.exp(jnp.sum(lp[2:3] * lp[3:4], axis=-1, keepdims=True)) + lam_init)
    vc = vc_ref[...]
    outs = []
    for mth in range(2):
        kl = None if kl_ref is None else kl_ref.at[mth]
        outs.append(_stream(q_ref[mth], kl, vl_ref, kc_ref[mth], vc, n_chunks, tk))
    y = outs[0] - lam * outs[1]
    ms = jnp.mean(y * y, axis=-1, keepdims=True)
    o_ref[...] = (y * lax.rsqrt(ms + NORM_EPS) * sub_ref[...] * (1.0 - lam_init)).astype(o_ref.dtype)


def _full_attention(kind, B, S, C, q, k, v, latent, extra=(), lam_init=0.0):
    maps = 2 if kind == "dif" else 1
    H = v.shape[0]
    dq, dv = q.shape[-1], v.shape[-1]
    nq = S if latent else C
    tq = _pick(nq, 256)
    tk = _pick(S, 512)
    per = nq // tq
    row0 = 0 if latent else B * S // tq
    ctx0 = B * S // C

    def lead(n):
        return (None,) if n == 1 else (n,)

    q_spec = pl.BlockSpec(lead(maps) + (tq, dq), lambda b, h, i: (h, row0 + b * per + i, 0))
    kc_spec = pl.BlockSpec(lead(maps) + (C, dq), lambda b, h, i: (h, ctx0 + b, 0))
    vc_spec = pl.BlockSpec((None, C, dv), lambda b, h, i: (h, ctx0 + b, 0))
    in_specs = [q_spec, kc_spec, vc_spec]
    args = [q, k, v]
    if latent:
        in_specs += [pl.BlockSpec(lead(maps) + (S, dq), lambda b, h, i: (h, b, 0)),
                     pl.BlockSpec((None, S, dv), lambda b, h, i: (h, b, 0))]
        args += [k, v]
    n_chunks = S // tk if latent else 0
    if kind == "dif":
        lam_p, sub_g = extra
        in_specs = [pl.BlockSpec(lam_p.shape, lambda b, h, i: (0, 0)),
                    pl.BlockSpec(sub_g.shape, lambda b, h, i: (0, 0))] + in_specs
        args = [lam_p, sub_g] + args
        body = functools.partial(_dif_kernel, n_chunks=n_chunks, tk=tk, lam_init=lam_init)
    else:
        body = functools.partial(_flash_kernel, n_chunks=n_chunks, tk=tk)
    return pl.pallas_call(
        body, grid=(B, H, per), in_specs=in_specs,
        out_specs=pl.BlockSpec((tq, dv), lambda b, h, i: (b * per + i, h)),
        out_shape=jax.ShapeDtypeStruct((B * nq, H * dv), BF16),
        compiler_params=_cparams("arbitrary", "arbitrary", "arbitrary"), name=kind + "_attention",
    )(*args)


def _swa_kernel(sink_ref, q_ref, kx_ref, vx_ref, *rest, window, n_blocks):
    g = pl.program_id(2)
    rep, blk, hd = q_ref.shape
    q = q_ref[...].reshape(rep * blk, hd)
    row = lax.broadcasted_iota(jnp.int32, (rep * blk, 1), 0)
    sink = jnp.zeros((rep * blk, 1), F32)
    for r in range(rep):
        sink = jnp.where(row // blk == r, sink_ref[g * rep + r] * LOG2E, sink)
    parts = [(_scores(q, kx_ref[...]), vx_ref[...])]
    if window:
        kp_ref, kc_ref, kn_ref, vp_ref, vc_ref, vn_ref, o_ref = rest
        n = pl.program_id(1)
        qi = lax.broadcasted_iota(jnp.int32, (rep * blk, blk), 0) % blk
        kj = lax.broadcasted_iota(jnp.int32, (rep * blk, blk), 1)
        sp = jnp.where((kj >= qi) & (n > 0), _scores(q, kp_ref[...]), NEG)
        sn = jnp.where((kj <= qi) & (n < n_blocks - 1), _scores(q, kn_ref[...]), NEG)
        parts += [(sp, vp_ref[...]), (_scores(q, kc_ref[...]), vc_ref[...]), (sn, vn_ref[...])]
    else:
        (o_ref,) = rest
    m = sink
    for s, _ in parts:
        m = jnp.maximum(m, jnp.max(s, axis=-1, keepdims=True))
    l = jnp.exp2(sink - m)
    acc = jnp.zeros((rep * blk, hd), F32)
    for s, v in parts:
        p = jnp.exp2(s - m)
        l = l + jnp.sum(p, axis=-1, keepdims=True)
        acc = acc + jnp.dot(p.astype(BF16), v, preferred_element_type=F32)
    out = (acc / l).astype(o_ref.dtype)
    for r in range(rep):
        o_ref[:, r * hd:(r + 1) * hd] = out[r * blk:(r + 1) * blk]


def _swa_attention(B, S, C, q, k, v, sink, latent):
    hd = q.shape[-1]
    blk = BLOCK
    nq = S if latent else C
    nb = nq // blk
    row0 = 0 if latent else B * S // blk
    ctx0 = B * S // C
    in_specs = [pl.BlockSpec(memory_space=pltpu.SMEM),
                pl.BlockSpec((SWA_REP, blk, hd), lambda b, n, g: (g, row0 + b * nb + n, 0)),
                pl.BlockSpec((None, C, hd), lambda b, n, g: (g, ctx0 + b, 0)),
                pl.BlockSpec((None, C, hd), lambda b, n, g: (g, ctx0 + b, 0))]
    args = [sink, q, k, v]
    if latent:
        band = [pl.BlockSpec((None, blk, hd), lambda b, n, g: (g, b * nb + jnp.maximum(n - 1, 0), 0)),
                pl.BlockSpec((None, blk, hd), lambda b, n, g: (g, b * nb + n, 0)),
                pl.BlockSpec((None, blk, hd), lambda b, n, g: (g, b * nb + jnp.minimum(n + 1, nb - 1), 0))]
        in_specs += band + band
        args += [k, k, k, v, v, v]
    return pl.pallas_call(
        functools.partial(_swa_kernel, window=latent, n_blocks=nb),
        grid=(B, nb, SWA_KV_HEADS), in_specs=in_specs,
        out_specs=pl.BlockSpec((blk, SWA_REP * hd), lambda b, n, g: (b * nb + n, g)),
        out_shape=jax.ShapeDtypeStruct((B * nq, SWA_HEADS * hd), BF16),
        compiler_params=_cparams("arbitrary", "arbitrary", "arbitrary"), name="swa_attention",
    )(*args)


def _rope_tables(S, dim, pad_rows):
    t = jnp.arange(S)
    quarter = dim // 4
    inv = ROPE_BASE ** (-jnp.arange(quarter, dtype=F32) / quarter)
    ang = jnp.concatenate([(t // GRID_W)[:, None].astype(F32) * inv,
                           (t % GRID_W)[:, None].astype(F32) * inv], axis=-1)
    half = LANE // 2
    padc = jnp.ones((S, half - dim // 2), F32)
    pads = jnp.zeros((S, half - dim // 2), F32)
    cos = jnp.concatenate([jnp.cos(ang), padc, jnp.cos(ang), padc], axis=-1)
    sin = jnp.concatenate([-jnp.sin(ang), pads, jnp.sin(ang), pads], axis=-1)
    cos = jnp.concatenate([cos, jnp.ones((pad_rows, LANE), F32)], axis=0)
    sin = jnp.concatenate([sin, jnp.zeros((pad_rows, LANE), F32)], axis=0)
    return cos, sin


def _spread_rope(a):
    h = MLA_ROPE // 2
    z = jnp.zeros(a.shape[:-1] + (LANE // 2 - h,), a.dtype)
    return jnp.concatenate([a[..., :h], z, a[..., h:], z], axis=-1)


def _pad_qk(a):
    return jnp.concatenate([a[..., :MLA_NOPE], _spread_rope(a[..., MLA_NOPE:])], axis=-1)


def _layout_w_in(w_in):
    sizes = (SWA_HEADS * HEAD_DIM, SWA_KV_HEADS * HEAD_DIM, SWA_KV_HEADS * HEAD_DIM,
             MLA_Q_RANK, MLA_KV_RANK, MLA_ROPE,
             DIF_HEADS * 2 * DIF_QK, DIF_HEADS * 2 * DIF_QK, DIF_HEADS * DIF_V)
    parts, start = [], 0
    for w in sizes:
        parts.append(w_in[..., start:start + w])
        start += w
    sq, sk, sv, cq, ckv, kr, dq, dk, dv = parts
    pad = jnp.zeros(w_in.shape[:-1] + ((U_END - U_KR - 1) * LANE,), w_in.dtype)
    return jnp.concatenate([cq, dq, ckv, sq, sk, sv, dk, dv, _spread_rope(kr), pad], axis=-1).astype(BF16)


def kernel(x, c, ctx, c_ctx, ada_w, ada_b, norm1_g, norm2_g, w_in, w_out, swa_sink, swa_q_norm, swa_k_norm, mla_cq_norm, mla_ckv_norm, mla_w_uq, mla_w_ukv, mla_q_norm, mla_k_norm, dif_lambda, dif_q_norm, dif_k_norm, dif_subln, ffn_w1, ffn_w3, ffn_w2, moe_router, moe_w1, moe_w3, moe_w2):
    B, S, D = x.shape
    C = ctx.shape[1]
    L = ada_w.shape[0]
    E = moe_router.shape[-1]
    rows = _Rows(B, S, C, 1024)
    T = rows.T

    w_in_p = _layout_w_in(w_in)
    w_out_b = w_out.astype(BF16)
    w_uq_p = _pad_qk(mla_w_uq.reshape(L, MLA_Q_RANK, MLA_HEADS, MLA_QK)).reshape(L, MLA_Q_RANK, -1).astype(BF16)
    w_ukv_b = mla_w_ukv.astype(BF16)
    ffn_w1b, ffn_w3b, ffn_w2b = ffn_w1.astype(BF16), ffn_w3.astype(BF16), ffn_w2.astype(BF16)
    moe_w1b, moe_w3b = moe_w1.astype(BF16), moe_w3.astype(BF16)
    moe_w2b = moe_w2.reshape(moe_w2.shape[0], -1, D).astype(BF16)
    router_p = jnp.pad(moe_router, ((0, 0), (0, 0), (0, LANE - E))).astype(BF16)
    rope_head = _rope_tables(S, HEAD_DIM, rows.tm)
    rope_mla = _rope_tables(S, MLA_ROPE, rows.tm)
    s_head = HEAD_DIM ** -0.5 * LOG2E
    s_mla = MLA_QK ** -0.5 * LOG2E

    cin = jnp.concatenate([c, c_ctx[None], jnp.zeros((MOD_ROWS - B - 1, D), F32)], axis=0)
    mod = _ada_mod(cin, ada_w, ada_b).reshape(L, MOD_ROWS * N_MOD, 1, D)

    h = jnp.concatenate([x.reshape(B * S, D), ctx.reshape(B * C, D)], axis=0)
    for l in range(L):
        modr = mod[l]
        lam_init = 0.8 - 0.6 * math.exp(-0.3 * l)

        xn = _norm_mod(rows, h, norm1_g[l], modr, 0, 1)
        z = _matmul(rows, xn, w_in_p[l], F32)

        sq = _headprep(rows, z, U_SQ, SWA_HEADS, LANE, jnp.tile(swa_q_norm[l] * s_head, SWA_HEADS)[None], rope_head)
        sk = _headprep(rows, z, U_SK, SWA_KV_HEADS, LANE, jnp.tile(swa_k_norm[l], SWA_KV_HEADS)[None], rope_head)
        sv = _headprep(rows, z, U_SV, SWA_KV_HEADS, LANE)
        dq = _headprep(rows, z, U_DQ, 2 * DIF_HEADS, LANE, jnp.tile(dif_q_norm[l].reshape(-1) * s_head, DIF_HEADS)[None], rope_head)
        dk = _headprep(rows, z, U_DK, 2 * DIF_HEADS, LANE, jnp.tile(dif_k_norm[l].reshape(-1), DIF_HEADS)[None], rope_head)
        dv = _headprep(rows, z, U_DV, DIF_HEADS, DIF_V)
        mq = _mla_q_prep(rows, z, mla_cq_norm[l][None], w_uq_p[l], _pad_qk(mla_q_norm[l] * s_mla)[None], rope_mla)
        mk, mv = _mla_kv_prep(rows, z, mla_ckv_norm[l][None], w_ukv_b[l], _pad_qk(mla_k_norm[l])[None], rope_mla)

        dif_extra = (dif_lambda[l], dif_subln[l][None])
        y_lat = jnp.concatenate([
            _swa_attention(B, S, C, sq, sk, sv, swa_sink[l], True),
            _full_attention("mla", B, S, C, mq, mk, mv, True),
            _full_attention("dif", B, S, C, dq, dk, dv, True, dif_extra, lam_init)], axis=1)
        y_ctx = jnp.concatenate([
            _swa_attention(B, S, C, sq, sk, sv, swa_sink[l], False),
            _full_attention("mla", B, S, C, mq, mk, mv, False),
            _full_attention("dif", B, S, C, dq, dk, dv, False, dif_extra, lam_init)], axis=1)
        y = jnp.concatenate([y_lat, y_ctx], axis=0)
        h = _matmul_residual(rows, y, w_out_b[l], h, modr, 2)

        i = l // 2
        if l % 2 == 0:
            hn = _norm_mod(rows, h, norm2_g[l], modr, 3, 4)
            hid = _swiglu(rows, hn, ffn_w1b[i], ffn_w3b[i])
            h = _matmul_residual(rows, hid, ffn_w2b[i], h, modr, 5)
        else:
            hn, comb = _norm_mod(rows, h, norm2_g[l], modr, 3, 4, router_p[i], E)
            hid = _swiglu(rows, hn, moe_w1b[i], moe_w3b[i], comb)
            h = _matmul_residual(rows, hid, moe_w2b[i], h, modr, 5)
    return h[:B * S].reshape(B, S, D)
```
